```python
import jax, jax.numpy as jnp
from jax import lax
import numpy as np

D_MODEL = 1024
BATCH = 16
SEQ = 2048
DEPTH = 4

CHUNK = 64
RWKV_HEAD_DIM = 64
RWKV_WIDTH = D_MODEL
RWKV_HEADS = RWKV_WIDTH // RWKV_HEAD_DIM
CONV_WIDTH = D_MODEL
CONV_KERNEL = 31
DECAY_LORA = D_MODEL // 16
A_LORA = D_MODEL // 16
V_LORA = D_MODEL // 32
G_LORA = D_MODEL // 8
D_FF = -(-8 * D_MODEL // (3 * 256)) * 256
N_IN_COLS = 3 * RWKV_WIDTH + 2 * CONV_WIDTH + 2 * D_MODEL
RMS_EPS = 1e-6
LN_EPS = 1e-5
GN_EPS = 64e-5

kernel_name = 'rwkv7_conformer_gated_hybrid'


def rms_norm(x, g):
    xf = x.astype(jnp.float32)
    y = xf * lax.rsqrt(jnp.mean(xf * xf, axis=-1, keepdims=True) + RMS_EPS)
    return (y * g.astype(jnp.float32)).astype(x.dtype)


def layer_norm(x, w, b):
    xf = x.astype(jnp.float32)
    mu = jnp.mean(xf, axis=-1, keepdims=True)
    var = jnp.mean(jnp.square(xf - mu), axis=-1, keepdims=True)
    y = (xf - mu) * lax.rsqrt(var + LN_EPS)
    return (y * w.astype(jnp.float32) + b.astype(jnp.float32)).astype(x.dtype)


def token_shift(x):
    return jnp.pad(x[:, :-1], ((0, 0), (1, 0), (0, 0)))


def lerp_shift(x, mu):
    return x + (token_shift(x) - x) * mu


def wkv7(r, w, k, v, a, b):
    B, T, H, N = r.shape
    seq = tuple(jnp.moveaxis(t.astype(jnp.float32), 1, 0) for t in (r, w, k, v, a, b))

    def step(S, inp):
        r_t, w_t, k_t, v_t, a_t, b_t = inp
        sa = jnp.einsum('bhvk,bhk->bhv', S, a_t)
        S = S * w_t[:, :, None, :] + sa[..., None] * b_t[:, :, None, :] + v_t[..., None] * k_t[:, :, None, :]
        y = jnp.einsum('bhvk,bhk->bhv', S, r_t)
        return S, y

    S0 = jnp.zeros((B, H, N, N), jnp.float32)
    _, ys = lax.scan(step, S0, seq)
    return jnp.moveaxis(ys, 0, 1)


def rwkv7_time_mix(h, p_r, p_k, p_v, v_first, vres, mu_rkv, mu_wag, w0, w1, w2, a0, a1, a2,
                   g1, g2, kk_scale, ka_scale, r_k, gn_w, gn_b, w_o):
    B, T, _ = h.shape
    xx = token_shift(h) - h
    xw = h + xx * mu_wag[0]
    xa = h + xx * mu_wag[1]
    xg = h + xx * mu_wag[2]
    r = lerp_shift(p_r, mu_rkv[0])
    k = lerp_shift(p_k, mu_rkv[1])
    v = lerp_shift(p_v, mu_rkv[2])
    w_log = -jax.nn.softplus(-(w0 + jnp.tanh(xw @ w1) @ w2).astype(jnp.float32)) - 0.5
    decay = jnp.exp(-jnp.exp(w_log))
    if vres is None:
        v_first = v
    else:
        mu_v, v0, v1, v2 = vres
        xv = h + xx * mu_v
        v = v + (v_first - v) * jax.nn.sigmoid(v0 + (xv @ v1) @ v2)
    a = jax.nn.sigmoid(a0 + (xa @ a1) @ a2)
    g = jax.nn.sigmoid(xg @ g1) @ g2

    def heads(t):
        return t.reshape(B, T, RWKV_HEADS, RWKV_HEAD_DIM)

    kk = heads(k * kk_scale).astype(jnp.float32)
    kk = kk / jnp.maximum(jnp.sqrt(jnp.sum(kk * kk, axis=-1, keepdims=True)), 1e-12)
    k = k * (1.0 + (a - 1.0) * ka_scale)
    rh, kh, vh, ah = heads(r), heads(k), heads(v), heads(a)
    y = wkv7(rh, heads(decay), kh, vh, -kk, kk * ah.astype(jnp.float32))
    mu = jnp.mean(y, axis=-1, keepdims=True)
    var = jnp.mean(jnp.square(y - mu), axis=-1, keepdims=True)
    y = ((y - mu) * lax.rsqrt(var + GN_EPS)).reshape(B, T, RWKV_WIDTH)
    y = y * gn_w.astype(jnp.float32) + gn_b.astype(jnp.float32)
    bonus = (jnp.sum((rh * kh * r_k).astype(jnp.float32), axis=-1, keepdims=True) * vh.astype(jnp.float32))
    y = ((y + bonus.reshape(B, T, RWKV_WIDTH)) * g.astype(jnp.float32)).astype(h.dtype)
    return y @ w_o, v_first


def conformer_conv(p_c, dw, dw_b, ln_w, ln_b, w_o):
    u, gate = jnp.split(p_c, 2, axis=-1)
    c = u * jax.nn.sigmoid(gate)
    c = lax.conv_general_dilated(c, dw[:, None, :].astype(c.dtype), window_strides=(1,),
                                 padding=[(CONV_KERNEL - 1, 0)],
                                 dimension_numbers=('NWC', 'WIO', 'NWC'),
                                 feature_group_count=CONV_WIDTH) + dw_b
    c = jax.nn.silu(layer_norm(c, ln_w, ln_b))
    return c @ w_o


def setup_inputs(seed: int = 0) -> dict:
    key = jax.random.key(seed)
    ks = iter(jax.random.split(key, 48))

    def nrm(shape, scale):
        return jax.random.normal(next(ks), shape, jnp.float32) * scale

    def unif(shape, lo, hi):
        return jax.random.uniform(next(ks), shape, jnp.float32, lo, hi)

    L, D, RW, CW = DEPTH, D_MODEL, RWKV_WIDTH, CONV_WIDTH
    Lv = DEPTH - 1
    return {
        'x': nrm((BATCH, SEQ, D), 1.0),
        'pre_mix_norm': 1.0 + nrm((L, D), 0.05),
        'post_mix_norm': 1.0 + nrm((L, D), 0.05),
        'pre_ffn_norm': 1.0 + nrm((L, D), 0.05),
        'post_ffn_norm': 1.0 + nrm((L, D), 0.05),
        'w_in': nrm((L, D, N_IN_COLS), D ** -0.5),
        'mu_rkv': unif((L, 3, RW), 0.0, 1.0),
        'mu_wag': unif((L, 3, D), 0.0, 1.0),
        'decay_w0': unif((L, RW), -6.0, -1.0),
        'decay_w1': nrm((L, D, DECAY_LORA), D ** -0.5),
        'decay_w2': nrm((L, DECAY_LORA, RW), 0.5 * DECAY_LORA ** -0.5),
        'a_0': nrm((L, RW), 0.5),
        'a_1': nrm((L, D, A_LORA), D ** -0.5),
        'a_2': nrm((L, A_LORA, RW), 0.5 * A_LORA ** -0.5),
        'g_1': nrm((L, D, G_LORA), D ** -0.5),
        'g_2': nrm((L, G_LORA, RW), G_LORA ** -0.5),
        'vres_mu': unif((Lv, D), 0.0, 1.0),
        'vres_0': nrm((Lv, RW), 0.5),
        'vres_1': nrm((Lv, D, V_LORA), D ** -0.5),
        'vres_2': nrm((Lv, V_LORA, RW), 0.5 * V_LORA ** -0.5),
        'k_k': 0.85 + nrm((L, RW), 0.05),
        'k_a': 1.0 + nrm((L, RW), 0.05),
        'r_k': nrm((L, RWKV_HEADS, RWKV_HEAD_DIM), 0.1),
        'gn_w': 1.0 + nrm((L, RW), 0.05),
        'gn_b': nrm((L, RW), 0.01),
        'w_rwkv_out': nrm((L, RW, D), RW ** -0.5),
        'conv_dw': nrm((L, CONV_KERNEL, CW), CONV_KERNEL ** -0.5),
        'conv_b': nrm((L, CW), 0.01),
        'conv_ln_w': 1.0 + nrm((L, CW), 0.05),
        'conv_ln_b': nrm((L, CW), 0.01),
        'w_conv_out': nrm((L, CW, D), CW ** -0.5),
        'w_out': nrm((L, D, D), D ** -0.5),
        'ffn_w_gate': nrm((L, D, D_FF), D ** -0.5),
        'ffn_w_up': nrm((L, D, D_FF), D ** -0.5),
        'ffn_w_down': nrm((L, D_FF, D), D_FF ** -0.5),
    }


def reference(x, pre_mix_norm, post_mix_norm, pre_ffn_norm, post_ffn_norm, w_in, mu_rkv, mu_wag,
              decay_w0, decay_w1, decay_w2, a_0, a_1, a_2, g_1, g_2, vres_mu, vres_0, vres_1, vres_2,
              k_k, k_a, r_k, gn_w, gn_b, w_rwkv_out, conv_dw, conv_b, conv_ln_w, conv_ln_b, w_conv_out,
              w_out, ffn_w_gate, ffn_w_up, ffn_w_down):
    splits = [RWKV_WIDTH, 2 * RWKV_WIDTH, 3 * RWKV_WIDTH,
              3 * RWKV_WIDTH + 2 * CONV_WIDTH, 3 * RWKV_WIDTH + 2 * CONV_WIDTH + D_MODEL]
    v_first = None
    for i in range(DEPTH):
        h = rms_norm(x, pre_mix_norm[i])
        proj = h @ w_in[i]
        p_r, p_k, p_v, p_c, z_rwkv, z_conv = jnp.split(proj, splits, axis=-1)
        vres = None if i == 0 else (vres_mu[i - 1], vres_0[i - 1], vres_1[i - 1], vres_2[i - 1])
        y_rwkv, v_first = rwkv7_time_mix(
            h, p_r, p_k, p_v, v_first, vres, mu_rkv[i], mu_wag[i],
            decay_w0[i], decay_w1[i], decay_w2[i], a_0[i], a_1[i], a_2[i], g_1[i], g_2[i],
            k_k[i], k_a[i], r_k[i], gn_w[i], gn_b[i], w_rwkv_out[i])
        y_conv = conformer_conv(p_c, conv_dw[i], conv_b[i], conv_ln_w[i], conv_ln_b[i], w_conv_out[i])
        merged = jax.nn.sigmoid(z_rwkv) * y_rwkv + jax.nn.sigmoid(z_conv) * y_conv
        x = x + rms_norm(merged @ w_out[i], post_mix_norm[i])
        h = rms_norm(x, pre_ffn_norm[i])
        f = (jax.nn.silu(h @ ffn_w_gate[i]) * (h @ ffn_w_up[i])) @ ffn_w_down[i]
        x = x + rms_norm(f, post_ffn_norm[i])
    return x
```

```python
import functools

import jax
import jax.numpy as jnp
from jax import lax
from jax.experimental import pallas as pl
from jax.experimental.pallas import tpu as pltpu

BF = jnp.bfloat16
F32 = jnp.float32

HEAD = 64
GROUP = 256
CHUNK = 64
CONV_K = 31
HALO = 32
RMS_EPS = 1e-6
LN_EPS = 1e-5
GN_EPS = 64e-5
VMEM_LIMIT = 56 * 1024 * 1024


def _dot(a, b):
    return jnp.dot(a, b, preferred_element_type=F32)


def _dot_nt(a, b):
    return lax.dot_general(a, b, (((1,), (1,)), ((), ())), preferred_element_type=F32)


def _dot_tn(a, b):
    return lax.dot_general(a, b, (((0,), (0,)), ((), ())), preferred_element_type=F32)


def _sigmoid(x):
    return 1.0 / (1.0 + jnp.exp(-x))


def _wkv_kernel(r_ref, k_ref, v_ref, kk_ref, b_ref, lw_ref, y_ref, s_ref, *, tile, n_groups):
    @pl.when(pl.program_id(1) == 0)
    def _():
        s_ref[...] = jnp.zeros_like(s_ref)

    row = lax.broadcasted_iota(jnp.int32, (CHUNK, GROUP), 0)
    col = lax.broadcasted_iota(jnp.int32, (CHUNK, GROUP), 1) % HEAD
    m_strict = col < row
    m_incl = col <= row
    eye = jnp.where(col == row, 1.0, 0.0).astype(F32)
    bdm = (lax.broadcasted_iota(jnp.int32, (GROUP, GROUP), 0) // HEAD
           == lax.broadcasted_iota(jnp.int32, (GROUP, GROUP), 1) // HEAD)
    tri = jnp.where(lax.broadcasted_iota(jnp.int32, (CHUNK, CHUNK), 0)
                    >= lax.broadcasted_iota(jnp.int32, (CHUNK, CHUNK), 1), 1.0, 0.0).astype(BF)

    def bd(x):
        xb = x.astype(BF)
        return jnp.where(bdm, jnp.concatenate([xb] * (GROUP // HEAD), axis=0), jnp.zeros((), BF))

    def rowform(full):
        z = jnp.where(bdm, full, 0.0)
        out = z[0:HEAD]
        for h in range(1, GROUP // HEAD):
            out = out + z[h * HEAD:(h + 1) * HEAD]
        return out

    def mm(a, b_bf):
        return _dot(a.astype(BF), b_bf)

    def stack(a, b):
        return jnp.concatenate([a, b], axis=0)

    def chunk_body(c, carry):
        rows = pl.ds(pl.multiple_of(c * CHUNK, CHUNK), CHUNK)
        for g in range(n_groups):
            cols = slice(g * GROUP, (g + 1) * GROUP)
            r = r_ref[0, rows, cols].astype(F32)
            k = k_ref[0, rows, cols].astype(F32)
            v = v_ref[0, rows, cols].astype(F32)
            kk = kk_ref[0, rows, cols].astype(F32)
            b = b_ref[0, rows, cols].astype(F32)
            lw = lw_ref[0, rows, cols]

            p0 = lw.astype(BF)
            r0 = lw - p0.astype(F32)
            p1 = r0.astype(BF)
            p2 = (r0 - p1.astype(F32)).astype(BF)
            cum = _dot(tri, p0) + _dot(tri, p1) + _dot(tri, p2)
            cum_end = cum[CHUNK - 1:CHUNK, :]
            e = jnp.exp(cum)
            einv = jnp.exp(-cum)
            eprev = jnp.exp(cum - lw)
            eend = jnp.exp(cum_end - cum)
            w_end = jnp.exp(cum_end)

            Rt = r * e
            At = -kk * eprev
            Kt = k * einv
            Bt = b * einv
            Kp = k * eend
            Bp = b * eend

            AR = stack(At, Rt).astype(BF)
            Gb = _dot_nt(AR, bd(Bt))
            Gk = _dot_nt(AR, bd(Kt))
            N = jnp.where(m_strict, Gb[:CHUNK], 0.0)
            Aak = jnp.where(m_strict, Gk[:CHUNK], 0.0)
            Arb = jnp.where(m_incl, Gb[CHUNK:], 0.0)
            Ark = jnp.where(m_incl, Gk[CHUNK:], 0.0)

            T = eye + N
            Pw = mm(N, bd(N))
            for _ in range(4):
                o = mm(stack(Pw, T), bd(Pw))
                Pw = o[:CHUNK]
                T = T + o[CHUNK:]
            T = T + mm(T, bd(Pw))

            o = mm(stack(Aak, Ark), bd(v))
            AakV = o[:CHUNK]
            ArkV = o[CHUNK:]
            Wt = mm(T, bd(At))
            Ut = mm(T, bd(AakV))
            Rhat = Rt + mm(Arb, bd(Wt))
            Yloc = mm(Arb, bd(Ut)) + ArkV

            PT = rowform(_dot_tn(Bp.astype(BF), Wt.astype(BF))) + eye * w_end
            QT = rowform(_dot_tn(stack(Bp, Kp).astype(BF), stack(Ut, v).astype(BF)))

            S = s_ref[g]
            o = mm(stack(PT, Rhat), bd(S))
            s_ref[g] = o[:CHUNK] + QT
            y_ref[0, rows, cols] = (o[CHUNK:] + Yloc).astype(y_ref.dtype)
        return carry

    lax.fori_loop(0, tile // CHUNK, chunk_body, 0)


def _wkv(r, k, v, kk, b, lw, *, tile=256):
    B, T, D = r.shape
    n_groups = D // GROUP
    spec = pl.BlockSpec((1, tile, D), lambda i, j: (i, j, 0))
    return pl.pallas_call(
        functools.partial(_wkv_kernel, tile=tile, n_groups=n_groups),
        grid=(B, T // tile),
        in_specs=[spec] * 6,
        out_specs=spec,
        out_shape=jax.ShapeDtypeStruct((B, T, D), F32),
        scratch_shapes=[pltpu.VMEM((n_groups, HEAD, GROUP), F32)],
        compiler_params=pltpu.CompilerParams(
            dimension_semantics=("arbitrary", "arbitrary"), vmem_limit_bytes=VMEM_LIMIT),
        name="wkv",
    )(r, k, v, kk, b, lw)


def _head_ones():
    return jnp.where(lax.broadcasted_iota(jnp.int32, (GROUP, GROUP), 0) // HEAD
                     == lax.broadcasted_iota(jnp.int32, (GROUP, GROUP), 1) // HEAD, 1.0, 0.0).astype(BF)


def _head_sum(q, ones_bd):
    hi = q.astype(BF)
    lo = (q - hi.astype(F32)).astype(BF)
    outs = []
    for g in range(q.shape[1] // GROUP):
        cols = slice(g * GROUP, (g + 1) * GROUP)
        outs.append(_dot(hi[:, cols], ones_bd) + _dot(lo[:, cols], ones_bd))
    return jnp.concatenate(outs, axis=1)


def _rms(x, gain):
    return x * lax.rsqrt(jnp.mean(x * x, axis=-1, keepdims=True) + RMS_EPS) * gain


(V_PRE, V_MU_R, V_MU_K, V_MU_V, V_MU_W, V_MU_A, V_MU_G, V_MU_VR, V_W0, V_A0, V_V0, V_KK, V_KA, V_RK,
 V_CB, V_LNW, V_LNB) = range(17)
N_VEC_IN = 24
V_GNW, V_GNB, V_POSTMIX, V_PREFFN, V_POSTFFN = range(5)
N_VEC_OUT = 8


def _mix_in_kernel(*refs, tile, d, has_vres):
    it = iter(refs)
    x_ref = next(it)
    vf_ref = next(it) if has_vres else None
    vec_ref, w_in_ref, w1_ref, w2_ref, a1_ref, a2_ref, g1_ref, g2_ref = (next(it) for _ in range(8))
    v1_ref, v2_ref = (next(it), next(it)) if has_vres else (None, None)
    dw_ref, wco_ref = next(it), next(it)
    (r_out, k_out, v_out, kk_out, b_out, lw_out, g_out, bonus_out, conv_out, zr_out) = (next(it) for _ in range(10))
    hprev_ref, pprev_ref, cbuf_ref, cacc_ref = (next(it) for _ in range(4))

    @pl.when(pl.program_id(1) == 0)
    def _():
        hprev_ref[...] = jnp.zeros_like(hprev_ref)
        pprev_ref[...] = jnp.zeros_like(pprev_ref)
        cbuf_ref[0:HALO, :] = jnp.zeros((HALO, d), F32)

    def vec(i):
        return vec_ref[i:i + 1, :]

    first_row = lax.broadcasted_iota(jnp.int32, (tile, 1), 0) == 0

    def shifted(cur, prev_row):
        return jnp.where(first_row, prev_row, pltpu.roll(cur, 1, axis=0))

    x = x_ref[0]
    h = _rms(x, vec(V_PRE))
    xx = shifted(h, hprev_ref[0:1, :]) - h
    hprev_ref[0:1, :] = h[tile - 1:tile, :]
    hb = h.astype(BF)

    p = _dot(hb, w_in_ref[:, 0:3 * d])
    pp = shifted(p, pprev_ref[0:1, :])
    pprev_ref[0:1, :] = p[tile - 1:tile, :]
    r = p[:, 0:d] + (pp[:, 0:d] - p[:, 0:d]) * vec(V_MU_R)
    k = p[:, d:2 * d] + (pp[:, d:2 * d] - p[:, d:2 * d]) * vec(V_MU_K)
    v = p[:, 2 * d:3 * d] + (pp[:, 2 * d:3 * d] - p[:, 2 * d:3 * d]) * vec(V_MU_V)

    def lora_in(mu_row, w_ref):
        return _dot((h + xx * vec(mu_row)).astype(BF), w_ref[...])

    z = vec(V_W0) + _dot(jnp.tanh(lora_in(V_MU_W, w1_ref)).astype(BF), w2_ref[...])
    softplus_neg = jnp.maximum(-z, 0.0) + jnp.log1p(jnp.exp(-jnp.abs(z)))
    lw_out[0] = -jnp.exp(-softplus_neg - 0.5)

    a = _sigmoid(vec(V_A0) + _dot(lora_in(V_MU_A, a1_ref).astype(BF), a2_ref[...]))
    g_out[0] = _dot(_sigmoid(lora_in(V_MU_G, g1_ref)).astype(BF), g2_ref[...]).astype(BF)
    if has_vres:
        mix = _sigmoid(vec(V_V0) + _dot(lora_in(V_MU_VR, v1_ref).astype(BF), v2_ref[...]))
        v = v + (vf_ref[0].astype(F32) - v) * mix

    ones_bd = _head_ones()
    kk = k * vec(V_KK)
    kk = kk / jnp.maximum(jnp.sqrt(_head_sum(kk * kk, ones_bd)), 1e-12)
    k = k * (1.0 + (a - 1.0) * vec(V_KA))
    r_out[0] = r.astype(BF)
    k_out[0] = k.astype(BF)
    v_out[0] = v.astype(BF)
    kk_out[0] = kk.astype(BF)
    b_out[0] = (kk * a).astype(BF)
    bonus_out[0] = (_head_sum(r * k * vec(V_RK), ones_bd) * v).astype(BF)

    pc = _dot(hb, w_in_ref[:, 3 * d:5 * d])
    cbuf_ref[HALO:HALO + tile, :] = pc[:, 0:d] * _sigmoid(pc[:, d:2 * d])
    rb = 32
    for i in range(tile // rb):
        acc = jnp.broadcast_to(vec(V_CB), (rb, d))
        for j in range(CONV_K):
            s = i * rb + HALO - (CONV_K - 1) + j
            acc = acc + dw_ref[j:j + 1, :] * cbuf_ref[s:s + rb, :]
        cacc_ref[i * rb:(i + 1) * rb, :] = acc
    cbuf_ref[0:HALO, :] = cbuf_ref[tile:tile + HALO, :]
    c = cacc_ref[...]
    mu = jnp.mean(c, axis=-1, keepdims=True)
    cz = c - mu
    var = jnp.mean(cz * cz, axis=-1, keepdims=True)
    cn = cz * lax.rsqrt(var + LN_EPS) * vec(V_LNW) + vec(V_LNB)
    y_conv = _dot((cn * _sigmoid(cn)).astype(BF), wco_ref[...])

    zz = _dot(hb, w_in_ref[:, 5 * d:7 * d])
    zr_out[0] = _sigmoid(zz[:, 0:d]).astype(BF)
    conv_out[0] = (_sigmoid(zz[:, d:2 * d]) * y_conv).astype(BF)


def _const_spec(shape):
    return pl.BlockSpec(shape, lambda i, j: (0,) * len(shape), pipeline_mode=pl.Buffered(1))


def _mix_in(x, v_first, vecs, w_in, loras, dw, w_conv_out, *, tile=256):
    B, T, D = x.shape
    has_vres = v_first is not None
    tok = pl.BlockSpec((1, tile, D), lambda i, j: (i, j, 0))
    args = [x] + ([v_first] if has_vres else []) + [vecs, w_in] + list(loras) + [dw, w_conv_out]
    in_specs = [tok] * (2 if has_vres else 1) + [_const_spec(a.shape) for a in args[(2 if has_vres else 1):]]
    out_shape = [jax.ShapeDtypeStruct((B, T, D), F32 if i == 5 else BF) for i in range(10)]
    return pl.pallas_call(
        functools.partial(_mix_in_kernel, tile=tile, d=D, has_vres=has_vres),
        grid=(B, T // tile),
        in_specs=in_specs,
        out_specs=[tok] * 10,
        out_shape=out_shape,
        scratch_shapes=[pltpu.VMEM((8, D), F32), pltpu.VMEM((8, 3 * D), F32),
                        pltpu.VMEM((tile + HALO, D), F32), pltpu.VMEM((tile, D), F32)],
        compiler_params=pltpu.CompilerParams(
            dimension_semantics=("arbitrary", "arbitrary"), vmem_limit_bytes=VMEM_LIMIT),
        name="mix_in",
    )(*args)


def _mix_out_kernel(x_ref, y_ref, g_ref, bonus_ref, conv_ref, zr_ref, vec_ref, wo_ref, wout_ref,
                    wg_ref, wu_ref, wd_ref, o_ref):
    def vec(i):
        return vec_ref[i:i + 1, :]

    ones_bd = _head_ones()
    y = y_ref[0]
    yc = y - _head_sum(y, ones_bd) * (1.0 / HEAD)
    var = _head_sum(yc * yc, ones_bd) * (1.0 / HEAD)
    yn = yc * lax.rsqrt(var + GN_EPS) * vec(V_GNW) + vec(V_GNB)
    y_rwkv = _dot(((yn + bonus_ref[0].astype(F32)) * g_ref[0].astype(F32)).astype(BF), wo_ref[...])
    merged = zr_ref[0].astype(F32) * y_rwkv + conv_ref[0].astype(F32)
    x1 = x_ref[0] + _rms(_dot(merged.astype(BF), wout_ref[...]), vec(V_POSTMIX))

    hb = _rms(x1, vec(V_PREFFN)).astype(BF)
    gate = _dot(hb, wg_ref[...])
    up = _dot(hb, wu_ref[...])
    f = _dot((gate * _sigmoid(gate) * up).astype(BF), wd_ref[...])
    o_ref[0] = x1 + _rms(f, vec(V_POSTFFN))


def _mix_out(x, y, g, bonus, convc, zr, vecs, w_o, w_out, w_gate, w_up, w_down, *, tile=256):
    B, T, D = x.shape
    tok = pl.BlockSpec((1, tile, D), lambda i, j: (i, j, 0))
    consts = [vecs, w_o, w_out, w_gate, w_up, w_down]
    return pl.pallas_call(
        _mix_out_kernel,
        grid=(B, T // tile),
        in_specs=[tok] * 6 + [_const_spec(a.shape) for a in consts],
        out_specs=tok,
        out_shape=jax.ShapeDtypeStruct((B, T, D), F32),
        compiler_params=pltpu.CompilerParams(
            dimension_semantics=("arbitrary", "arbitrary"), vmem_limit_bytes=VMEM_LIMIT),
        name="mix_out",
    )(x, y, g, bonus, convc, zr, *consts)


def _rows(rows, n, d):
    rows = [jnp.reshape(r, (1, d)).astype(F32) for r in rows]
    return jnp.concatenate(rows + [jnp.zeros((n - len(rows), d), F32)], axis=0)


def kernel(x, pre_mix_norm, post_mix_norm, pre_ffn_norm, post_ffn_norm, w_in, mu_rkv, mu_wag, decay_w0, decay_w1, decay_w2, a_0, a_1, a_2, g_1, g_2, vres_mu, vres_0, vres_1, vres_2, k_k, k_a, r_k, gn_w, gn_b, w_rwkv_out, conv_dw, conv_b, conv_ln_w, conv_ln_b, w_conv_out, w_out, ffn_w_gate, ffn_w_up, ffn_w_down):
    depth, d = pre_mix_norm.shape
    bf = lambda w: w.astype(BF)
    v_first = None
    for i in range(depth):
        has_vres = i > 0
        zero = jnp.zeros((d,), F32)
        vec_in = _rows([pre_mix_norm[i], mu_rkv[i, 0], mu_rkv[i, 1], mu_rkv[i, 2],
                        mu_wag[i, 0], mu_wag[i, 1], mu_wag[i, 2],
                        vres_mu[i - 1] if has_vres else zero, decay_w0[i], a_0[i],
                        vres_0[i - 1] if has_vres else zero, k_k[i], k_a[i], r_k[i],
                        conv_b[i], conv_ln_w[i], conv_ln_b[i]], N_VEC_IN, d)
        loras = [bf(decay_w1[i]), bf(decay_w2[i]), bf(a_1[i]), bf(a_2[i]), bf(g_1[i]), bf(g_2[i])]
        if has_vres:
            loras += [bf(vres_1[i - 1]), bf(vres_2[i - 1])]
        r, k, v, kk, b, lw, g, bonus, convc, zr = _mix_in(
            x, v_first, vec_in, bf(w_in[i]), loras, conv_dw[i], bf(w_conv_out[i]))
        if not has_vres:
            v_first = v
        y = _wkv(r, k, v, kk, b, lw)
        vec_out = _rows([gn_w[i], gn_b[i], post_mix_norm[i], pre_ffn_norm[i], post_ffn_norm[i]], N_VEC_OUT, d)
        x = _mix_out(x, y, g, bonus, convc, zr, vec_out, bf(w_rwkv_out[i]), bf(w_out[i]),
                     bf(ffn_w_gate[i]), bf(ffn_w_up[i]), bf(ffn_w_down[i]))
    return x
```

```python
import functools

import jax
import jax.numpy as jnp
from jax import lax
from jax.experimental import pallas as pl
from jax.experimental.pallas import tpu as pltpu

BF = jnp.bfloat16
F32 = jnp.float32

HEAD = 64
GROUP = 256
CHUNK = 64
CONV_K = 31
SUBLANES = 8
HALO = 32
RMS_EPS = 1e-6
LN_EPS = 1e-5
GN_EPS = 64e-5
VMEM_LIMIT = 56 * 1024 * 1024


def _dot(a, b):
    return jnp.dot(a, b, preferred_element_type=F32)


def _dot_nt(a, b):
    return lax.dot_general(a, b, (((1,), (1,)), ((), ())), preferred_element_type=F32)


def _dot_tn(a, b):
    return lax.dot_general(a, b, (((0,), (0,)), ((), ())), preferred_element_type=F32)


def _sigmoid(x):
    return 1.0 / (1.0 + jnp.exp(-x))


def _wkv_kernel(r_ref, k_ref, v_ref, kk_ref, b_ref, lw_ref, y_ref, s_ref, *, tile, n_groups,
                chunks_per_iter):
    @pl.when(pl.program_id(1) == 0)
    def _():
        s_ref[...] = jnp.zeros_like(s_ref)

    row = lax.broadcasted_iota(jnp.int32, (CHUNK, GROUP), 0)
    col = lax.broadcasted_iota(jnp.int32, (CHUNK, GROUP), 1) % HEAD
    m_strict = col < row
    m_incl = col <= row
    eye = jnp.where(col == row, 1.0, 0.0).astype(F32)
    bdm = (lax.broadcasted_iota(jnp.int32, (GROUP, GROUP), 0) // HEAD
           == lax.broadcasted_iota(jnp.int32, (GROUP, GROUP), 1) // HEAD)
    tri = jnp.where(lax.broadcasted_iota(jnp.int32, (CHUNK, CHUNK), 0)
                    >= lax.broadcasted_iota(jnp.int32, (CHUNK, CHUNK), 1), 1.0, 0.0).astype(BF)

    def bd(x):
        xb = x.astype(BF)
        return jnp.where(bdm, jnp.concatenate([xb] * (GROUP // HEAD), axis=0), jnp.zeros((), BF))

    def rowform(full):
        z = jnp.where(bdm, full, 0.0)
        out = z[0:HEAD]
        for h in range(1, GROUP // HEAD):
            out = out + z[h * HEAD:(h + 1) * HEAD]
        return out

    def mm(a, b_bf):
        return _dot(a.astype(BF), b_bf)

    def stack(a, b):
        return jnp.concatenate([a, b], axis=0)

    def chunk_body(it, carry):
        chains = [(j, g) for j in range(chunks_per_iter) for g in range(n_groups)]

        def rows_of(j):
            return pl.ds(pl.multiple_of((it * chunks_per_iter + j) * CHUNK, CHUNK), CHUNK)

        def load(ref):
            return [ref[0, rows_of(j), g * GROUP:(g + 1) * GROUP] for j, g in chains]

        lw = load(lw_ref)
        p0 = [x.astype(BF) for x in lw]
        r0 = [x - p.astype(F32) for x, p in zip(lw, p0)]
        p1 = [x.astype(BF) for x in r0]
        p2 = [(x - p.astype(F32)).astype(BF) for x, p in zip(r0, p1)]
        cum = [_dot(tri, a) + _dot(tri, b) + _dot(tri, c) for a, b, c in zip(p0, p1, p2)]
        cum_end = [x[CHUNK - 1:CHUNK, :] for x in cum]
        r = [x.astype(F32) for x in load(r_ref)]
        k = [x.astype(F32) for x in load(k_ref)]
        v = load(v_ref)
        kk = [x.astype(F32) for x in load(kk_ref)]
        b = [x.astype(F32) for x in load(b_ref)]
        einv = [jnp.exp(-x) for x in cum]
        eend = [jnp.exp(ce - x) for x, ce in zip(cum, cum_end)]
        Rt = [x * jnp.exp(c) for x, c in zip(r, cum)]
        At = [-x * jnp.exp(c - l) for x, c, l in zip(kk, cum, lw)]
        Kt = [x * e for x, e in zip(k, einv)]
        Bt = [x * e for x, e in zip(b, einv)]
        Kp = [x * e for x, e in zip(k, eend)]
        Bp = [x * e for x, e in zip(b, eend)]

        AR = [stack(a, x).astype(BF) for a, x in zip(At, Rt)]
        Gb = [_dot_nt(a, bd(x)) for a, x in zip(AR, Bt)]
        Gk = [_dot_nt(a, bd(x)) for a, x in zip(AR, Kt)]
        N = [jnp.where(m_strict, x[:CHUNK], 0.0) for x in Gb]
        Aak = [jnp.where(m_strict, x[:CHUNK], 0.0) for x in Gk]
        Arb = [jnp.where(m_incl, x[CHUNK:], 0.0) for x in Gb]
        Ark = [jnp.where(m_incl, x[CHUNK:], 0.0) for x in Gk]

        T = [eye + x for x in N]
        Pw = [mm(x, bd(x)) for x in N]
        for _ in range(4):
            o = [mm(stack(p, t), bd(p)) for p, t in zip(Pw, T)]
            Pw = [x[:CHUNK] for x in o]
            T = [t + x[CHUNK:] for t, x in zip(T, o)]
        T = [t + mm(t, bd(p)) for t, p in zip(T, Pw)]

        o = [mm(stack(a, x), bd(vv)) for a, x, vv in zip(Aak, Ark, v)]
        AakV = [x[:CHUNK] for x in o]
        ArkV = [x[CHUNK:] for x in o]
        Wt = [mm(t, bd(a)) for t, a in zip(T, At)]
        Ut = [mm(t, bd(a)) for t, a in zip(T, AakV)]
        Rhat = [x + mm(a, bd(w)) for x, a, w in zip(Rt, Arb, Wt)]
        Yloc = [mm(a, bd(u)) + x for a, u, x in zip(Arb, Ut, ArkV)]
        PT = [rowform(_dot_tn(x.astype(BF), w.astype(BF))) + eye * jnp.exp(ce)
              for x, w, ce in zip(Bp, Wt, cum_end)]
        QT = [rowform(_dot_tn(stack(x, y).astype(BF), stack(u.astype(BF), vv)))
              for x, y, u, vv in zip(Bp, Kp, Ut, v)]
        lhs = [stack(p, x).astype(BF) for p, x in zip(PT, Rhat)]

        for i, (j, g) in enumerate(chains):
            o = _dot(lhs[i], bd(s_ref[g]))
            s_ref[g] = o[:CHUNK] + QT[i]
            y_ref[0, rows_of(j), g * GROUP:(g + 1) * GROUP] = (o[CHUNK:] + Yloc[i]).astype(y_ref.dtype)
        return carry

    lax.fori_loop(0, tile // (CHUNK * chunks_per_iter), chunk_body, 0)


def _wkv(r, k, v, kk, b, lw, *, tile=256, chunks_per_iter=2):
    B, T, D = r.shape
    n_groups = D // GROUP
    spec = pl.BlockSpec((1, tile, D), lambda i, j: (i, j, 0))
    return pl.pallas_call(
        functools.partial(_wkv_kernel, tile=tile, n_groups=n_groups, chunks_per_iter=chunks_per_iter),
        grid=(B, T // tile),
        in_specs=[spec] * 6,
        out_specs=spec,
        out_shape=jax.ShapeDtypeStruct((B, T, D), F32),
        scratch_shapes=[pltpu.VMEM((n_groups, HEAD, GROUP), F32)],
        compiler_params=pltpu.CompilerParams(
            dimension_semantics=("arbitrary", "arbitrary"), vmem_limit_bytes=VMEM_LIMIT),
        name="wkv",
    )(r, k, v, kk, b, lw)


def _head_ones():
    return jnp.where(lax.broadcasted_iota(jnp.int32, (GROUP, GROUP), 0) // HEAD
                     == lax.broadcasted_iota(jnp.int32, (GROUP, GROUP), 1) // HEAD, 1.0, 0.0).astype(BF)


def _head_sum(q, ones_bd):
    hi = q.astype(BF)
    lo = (q - hi.astype(F32)).astype(BF)
    outs = []
    for g in range(q.shape[1] // GROUP):
        cols = slice(g * GROUP, (g + 1) * GROUP)
        outs.append(_dot(hi[:, cols], ones_bd) + _dot(lo[:, cols], ones_bd))
    return jnp.concatenate(outs, axis=1)


def _rms(x, gain):
    return x * lax.rsqrt(jnp.mean(x * x, axis=-1, keepdims=True) + RMS_EPS) * gain


(V_PRE, V_MU_R, V_MU_K, V_MU_V, V_MU_W, V_MU_A, V_MU_G, V_MU_VR, V_W0, V_A0, V_V0, V_KK, V_KA, V_RK,
 V_CB, V_LNW, V_LNB) = range(17)
N_VEC_IN = 24
V_GNW, V_GNB, V_POSTMIX, V_PREFFN, V_POSTFFN = range(5)
N_VEC_OUT = 8


def _mix_in_kernel(*refs, tile, d, has_vres):
    it = iter(refs)
    x_ref = next(it)
    vf_ref = next(it) if has_vres else None
    vec_ref, w_in_ref, w1_ref, w2_ref, a1_ref, a2_ref, g1_ref, g2_ref = (next(it) for _ in range(8))
    v1_ref, v2_ref = (next(it), next(it)) if has_vres else (None, None)
    dw_ref, wco_ref = next(it), next(it)
    (r_out, k_out, v_out, kk_out, b_out, lw_out, g_out, bonus_out, conv_out, zr_out) = (next(it) for _ in range(10))
    hprev_ref, pprev_ref, cbuf_ref, csh_ref, cacc_ref = (next(it) for _ in range(5))

    @pl.when(pl.program_id(1) == 0)
    def _():
        hprev_ref[...] = jnp.zeros_like(hprev_ref)
        pprev_ref[...] = jnp.zeros_like(pprev_ref)
        cbuf_ref[0:HALO, :] = jnp.zeros((HALO, d), F32)

    def vec(i):
        return vec_ref[i:i + 1, :]

    first_row = lax.broadcasted_iota(jnp.int32, (tile, 1), 0) == 0

    def shifted(cur, prev_row):
        return jnp.where(first_row, prev_row, pltpu.roll(cur, 1, axis=0))

    x = x_ref[0]
    h = _rms(x, vec(V_PRE))
    xx = shifted(h, hprev_ref[0:1, :]) - h
    hprev_ref[0:1, :] = h[tile - 1:tile, :]
    hb = h.astype(BF)

    p = _dot(hb, w_in_ref[:, 0:3 * d])
    pp = shifted(p, pprev_ref[0:1, :])
    pprev_ref[0:1, :] = p[tile - 1:tile, :]
    r = p[:, 0:d] + (pp[:, 0:d] - p[:, 0:d]) * vec(V_MU_R)
    k = p[:, d:2 * d] + (pp[:, d:2 * d] - p[:, d:2 * d]) * vec(V_MU_K)
    v = p[:, 2 * d:3 * d] + (pp[:, 2 * d:3 * d] - p[:, 2 * d:3 * d]) * vec(V_MU_V)

    def lora_in(mu_row, w_ref):
        return _dot((h + xx * vec(mu_row)).astype(BF), w_ref[...])

    z = vec(V_W0) + _dot(jnp.tanh(lora_in(V_MU_W, w1_ref)).astype(BF), w2_ref[...])
    softplus_neg = jnp.maximum(-z, 0.0) + jnp.log1p(jnp.exp(-jnp.abs(z)))
    lw_out[0] = -jnp.exp(-softplus_neg - 0.5)

    a = _sigmoid(vec(V_A0) + _dot(lora_in(V_MU_A, a1_ref).astype(BF), a2_ref[...]))
    g_out[0] = _dot(_sigmoid(lora_in(V_MU_G, g1_ref)).astype(BF), g2_ref[...]).astype(BF)
    if has_vres:
        mix = _sigmoid(vec(V_V0) + _dot(lora_in(V_MU_VR, v1_ref).astype(BF), v2_ref[...]))
        v = v + (vf_ref[0].astype(F32) - v) * mix

    ones_bd = _head_ones()
    kk = k * vec(V_KK)
    kk = kk / jnp.maximum(jnp.sqrt(_head_sum(kk * kk, ones_bd)), 1e-12)
    k = k * (1.0 + (a - 1.0) * vec(V_KA))
    r_out[0] = r.astype(BF)
    k_out[0] = k.astype(BF)
    v_out[0] = v.astype(BF)
    kk_out[0] = kk.astype(BF)
    b_out[0] = (kk * a).astype(BF)
    bonus_out[0] = (_head_sum(r * k * vec(V_RK), ones_bd) * v).astype(BF)

    pc = _dot(hb, w_in_ref[:, 3 * d:5 * d])
    cbuf_ref[HALO:HALO + tile, :] = pc[:, 0:d] * _sigmoid(pc[:, d:2 * d])
    cfull = cbuf_ref[...]
    for m in range(1, SUBLANES):
        csh_ref[m - 1] = pltpu.roll(cfull, m, axis=0)
    rb = 32
    for i in range(tile // rb):
        acc = jnp.broadcast_to(vec(V_CB), (rb, d))
        for j in range(CONV_K):
            q, m = divmod(CONV_K - 1 - j, SUBLANES)
            s = HALO + i * rb - q * SUBLANES
            src = cbuf_ref[s:s + rb, :] if m == 0 else csh_ref[m - 1, s:s + rb, :]
            acc = acc + dw_ref[j:j + 1, :] * src
        cacc_ref[i * rb:(i + 1) * rb, :] = acc
    cbuf_ref[0:HALO, :] = cbuf_ref[tile:tile + HALO, :]
    c = cacc_ref[...]
    mu = jnp.mean(c, axis=-1, keepdims=True)
    cz = c - mu
    var = jnp.mean(cz * cz, axis=-1, keepdims=True)
    cn = cz * lax.rsqrt(var + LN_EPS) * vec(V_LNW) + vec(V_LNB)
    y_conv = _dot((cn * _sigmoid(cn)).astype(BF), wco_ref[...])

    zz = _dot(hb, w_in_ref[:, 5 * d:7 * d])
    zr_out[0] = _sigmoid(zz[:, 0:d]).astype(BF)
    conv_out[0] = (_sigmoid(zz[:, d:2 * d]) * y_conv).astype(BF)


def _const_spec(shape):
    return pl.BlockSpec(shape, lambda i, j: (0,) * len(shape), pipeline_mode=pl.Buffered(1))


def _mix_in(x, v_first, vecs, w_in, loras, dw, w_conv_out, *, tile=256):
    B, T, D = x.shape
    has_vres = v_first is not None
    tok = pl.BlockSpec((1, tile, D), lambda i, j: (i, j, 0))
    args = [x] + ([v_first] if has_vres else []) + [vecs, w_in] + list(loras) + [dw, w_conv_out]
    in_specs = [tok] * (2 if has_vres else 1) + [_const_spec(a.shape) for a in args[(2 if has_vres else 1):]]
    out_shape = [jax.ShapeDtypeStruct((B, T, D), F32 if i == 5 else BF) for i in range(10)]
    return pl.pallas_call(
        functools.partial(_mix_in_kernel, tile=tile, d=D, has_vres=has_vres),
        grid=(B, T // tile),
        in_specs=in_specs,
        out_specs=[tok] * 10,
        out_shape=out_shape,
        scratch_shapes=[pltpu.VMEM((8, D), F32), pltpu.VMEM((8, 3 * D), F32),
                        pltpu.VMEM((tile + HALO, D), F32), pltpu.VMEM((SUBLANES - 1, tile + HALO, D), F32),
                        pltpu.VMEM((tile, D), F32)],
        compiler_params=pltpu.CompilerParams(
            dimension_semantics=("arbitrary", "arbitrary"), vmem_limit_bytes=VMEM_LIMIT),
        name="mix_in",
    )(*args)


def _mix_out_kernel(x_ref, y_ref, g_ref, bonus_ref, conv_ref, zr_ref, vec_ref, wo_ref, wout_ref,
                    wg_ref, wu_ref, wd_ref, o_ref):
    def vec(i):
        return vec_ref[i:i + 1, :]

    ones_bd = _head_ones()
    y = y_ref[0]
    yc = y - _head_sum(y, ones_bd) * (1.0 / HEAD)
    var = _head_sum(yc * yc, ones_bd) * (1.0 / HEAD)
    yn = yc * lax.rsqrt(var + GN_EPS) * vec(V_GNW) + vec(V_GNB)
    y_rwkv = _dot(((yn + bonus_ref[0].astype(F32)) * g_ref[0].astype(F32)).astype(BF), wo_ref[...])
    merged = zr_ref[0].astype(F32) * y_rwkv + conv_ref[0].astype(F32)
    x1 = x_ref[0] + _rms(_dot(merged.astype(BF), wout_ref[...]), vec(V_POSTMIX))

    hb = _rms(x1, vec(V_PREFFN)).astype(BF)
    gate = _dot(hb, wg_ref[...])
    up = _dot(hb, wu_ref[...])
    f = _dot((gate * _sigmoid(gate) * up).astype(BF), wd_ref[...])
    o_ref[0] = x1 + _rms(f, vec(V_POSTFFN))


def _mix_out(x, y, g, bonus, convc, zr, vecs, w_o, w_out, w_gate, w_up, w_down, *, tile=256):
    B, T, D = x.shape
    tok = pl.BlockSpec((1, tile, D), lambda i, j: (i, j, 0))
    consts = [vecs, w_o, w_out, w_gate, w_up, w_down]
    return pl.pallas_call(
        _mix_out_kernel,
        grid=(B, T // tile),
        in_specs=[tok] * 6 + [_const_spec(a.shape) for a in consts],
        out_specs=tok,
        out_shape=jax.ShapeDtypeStruct((B, T, D), F32),
        compiler_params=pltpu.CompilerParams(
            dimension_semantics=("arbitrary", "arbitrary"), vmem_limit_bytes=VMEM_LIMIT),
        name="mix_out",
    )(x, y, g, bonus, convc, zr, *consts)


def _rows(rows, n, d):
    rows = [jnp.reshape(r, (1, d)).astype(F32) for r in rows]
    return jnp.concatenate(rows + [jnp.zeros((n - len(rows), d), F32)], axis=0)


def kernel(x, pre_mix_norm, post_mix_norm, pre_ffn_norm, post_ffn_norm, w_in, mu_rkv, mu_wag, decay_w0, decay_w1, decay_w2, a_0, a_1, a_2, g_1, g_2, vres_mu, vres_0, vres_1, vres_2, k_k, k_a, r_k, gn_w, gn_b, w_rwkv_out, conv_dw, conv_b, conv_ln_w, conv_ln_b, w_conv_out, w_out, ffn_w_gate, ffn_w_up, ffn_w_down):
    depth, d = pre_mix_norm.shape
    bf = lambda w: w.astype(BF)
    v_first = None
    for i in range(depth):
        has_vres = i > 0
        zero = jnp.zeros((d,), F32)
        vec_in = _rows([pre_mix_norm[i], mu_rkv[i, 0], mu_rkv[i, 1], mu_rkv[i, 2],
                        mu_wag[i, 0], mu_wag[i, 1], mu_wag[i, 2],
                        vres_mu[i - 1] if has_vres else zero, decay_w0[i], a_0[i],
                        vres_0[i - 1] if has_vres else zero, k_k[i], k_a[i], r_k[i],
                        conv_b[i], conv_ln_w[i], conv_ln_b[i]], N_VEC_IN, d)
        loras = [bf(decay_w1[i]), bf(decay_w2[i]), bf(a_1[i]), bf(a_2[i]), bf(g_1[i]), bf(g_2[i])]
        if has_vres:
            loras += [bf(vres_1[i - 1]), bf(vres_2[i - 1])]
        r, k, v, kk, b, lw, g, bonus, convc, zr = _mix_in(
            x, v_first, vec_in, bf(w_in[i]), loras, conv_dw[i], bf(w_conv_out[i]))
        if not has_vres:
            v_first = v
        y = _wkv(r, k, v, kk, b, lw)
        vec_out = _rows([gn_w[i], gn_b[i], post_mix_norm[i], pre_ffn_norm[i], post_ffn_norm[i]], N_VEC_OUT, d)
        x = _mix_out(x, y, g, bonus, convc, zr, vec_out, bf(w_rwkv_out[i]), bf(w_out[i]),
                     bf(ffn_w_gate[i]), bf(ffn_w_up[i]), bf(ffn_w_down[i]))
    return x
```

```python
import functools

import jax
import jax.numpy as jnp
from jax import lax
from jax.experimental import pallas as pl
from jax.experimental.pallas import tpu as pltpu

BF = jnp.bfloat16
F32 = jnp.float32

HEAD = 64
GROUP = 256
CHUNK = 64
CONV_K = 31
SUBLANES = 8
FFN_CHUNK = 768
HALO = 32
RMS_EPS = 1e-6
LN_EPS = 1e-5
GN_EPS = 64e-5
VMEM_LIMIT = 56 * 1024 * 1024


def _dot(a, b):
    return jnp.dot(a, b, preferred_element_type=F32)


def _dot_nt(a, b):
    return lax.dot_general(a, b, (((1,), (1,)), ((), ())), preferred_element_type=F32)


def _dot_tn(a, b):
    return lax.dot_general(a, b, (((0,), (0,)), ((), ())), preferred_element_type=F32)


def _sigmoid(x):
    return 1.0 / (1.0 + jnp.exp(-x))


def _wkv_kernel(r_ref, k_ref, v_ref, kk_ref, b_ref, lw_ref, y_ref, s_ref, *, tile, n_groups,
                chunks_per_iter):
    @pl.when(pl.program_id(1) == 0)
    def _():
        s_ref[...] = jnp.zeros_like(s_ref)

    row = lax.broadcasted_iota(jnp.int32, (CHUNK, GROUP), 0)
    col = lax.broadcasted_iota(jnp.int32, (CHUNK, GROUP), 1) % HEAD
    m_strict = col < row
    m_incl = col <= row
    eye = jnp.where(col == row, 1.0, 0.0).astype(F32)
    bdm = (lax.broadcasted_iota(jnp.int32, (GROUP, GROUP), 0) // HEAD
           == lax.broadcasted_iota(jnp.int32, (GROUP, GROUP), 1) // HEAD)
    tri = jnp.where(lax.broadcasted_iota(jnp.int32, (CHUNK, CHUNK), 0)
                    >= lax.broadcasted_iota(jnp.int32, (CHUNK, CHUNK), 1), 1.0, 0.0).astype(BF)

    def bd(x):
        xb = x.astype(BF)
        return jnp.where(bdm, jnp.concatenate([xb] * (GROUP // HEAD), axis=0), jnp.zeros((), BF))

    def rowform(full):
        z = jnp.where(bdm, full, 0.0)
        out = z[0:HEAD]
        for h in range(1, GROUP // HEAD):
            out = out + z[h * HEAD:(h + 1) * HEAD]
        return out

    def mm(a, b_bf):
        return _dot(a.astype(BF), b_bf)

    def stack(a, b):
        return jnp.concatenate([a, b], axis=0)

    def chunk_body(it, carry):
        chains = [(j, g) for j in range(chunks_per_iter) for g in range(n_groups)]

        def rows_of(j):
            return pl.ds(pl.multiple_of((it * chunks_per_iter + j) * CHUNK, CHUNK), CHUNK)

        def load(ref):
            return [ref[0, rows_of(j), g * GROUP:(g + 1) * GROUP] for j, g in chains]

        lw = load(lw_ref)
        p0 = [x.astype(BF) for x in lw]
        r0 = [x - p.astype(F32) for x, p in zip(lw, p0)]
        p1 = [x.astype(BF) for x in r0]
        p2 = [(x - p.astype(F32)).astype(BF) for x, p in zip(r0, p1)]
        cum = [_dot(tri, a) + _dot(tri, b) + _dot(tri, c) for a, b, c in zip(p0, p1, p2)]
        cum_end = [x[CHUNK - 1:CHUNK, :] for x in cum]
        r = [x.astype(F32) for x in load(r_ref)]
        k = [x.astype(F32) for x in load(k_ref)]
        v = load(v_ref)
        kk = [x.astype(F32) for x in load(kk_ref)]
        b = [x.astype(F32) for x in load(b_ref)]
        einv = [jnp.exp(-x) for x in cum]
        eend = [jnp.exp(ce - x) for x, ce in zip(cum, cum_end)]
        Rt = [x * jnp.exp(c) for x, c in zip(r, cum)]
        At = [-x * jnp.exp(c - l) for x, c, l in zip(kk, cum, lw)]
        Kt = [x * e for x, e in zip(k, einv)]
        Bt = [x * e for x, e in zip(b, einv)]
        Kp = [x * e for x, e in zip(k, eend)]
        Bp = [x * e for x, e in zip(b, eend)]

        AR = [stack(a, x).astype(BF) for a, x in zip(At, Rt)]
        Gb = [_dot_nt(a, bd(x)) for a, x in zip(AR, Bt)]
        Gk = [_dot_nt(a, bd(x)) for a, x in zip(AR, Kt)]
        N = [jnp.where(m_strict, x[:CHUNK], 0.0) for x in Gb]
        Aak = [jnp.where(m_strict, x[:CHUNK], 0.0) for x in Gk]
        Arb = [jnp.where(m_incl, x[CHUNK:], 0.0) for x in Gb]
        Ark = [jnp.where(m_incl, x[CHUNK:], 0.0) for x in Gk]

        T = [eye + x for x in N]
        Pw = [mm(x, bd(x)) for x in N]
        for _ in range(4):
            o = [mm(stack(p, t), bd(p)) for p, t in zip(Pw, T)]
            Pw = [x[:CHUNK] for x in o]
            T = [t + x[CHUNK:] for t, x in zip(T, o)]
        T = [t + mm(t, bd(p)) for t, p in zip(T, Pw)]

        o = [mm(stack(a, x), bd(vv)) for a, x, vv in zip(Aak, Ark, v)]
        AakV = [x[:CHUNK] for x in o]
        ArkV = [x[CHUNK:] for x in o]
        Wt = [mm(t, bd(a)) for t, a in zip(T, At)]
        Ut = [mm(t, bd(a)) for t, a in zip(T, AakV)]
        Rhat = [x + mm(a, bd(w)) for x, a, w in zip(Rt, Arb, Wt)]
        Yloc = [mm(a, bd(u)) + x for a, u, x in zip(Arb, Ut, ArkV)]
        PT = [rowform(_dot_tn(x.astype(BF), w.astype(BF))) + eye * jnp.exp(ce)
              for x, w, ce in zip(Bp, Wt, cum_end)]
        QT = [rowform(_dot_tn(stack(x, y).astype(BF), stack(u.astype(BF), vv)))
              for x, y, u, vv in zip(Bp, Kp, Ut, v)]
        lhs = [stack(p, x).astype(BF) for p, x in zip(PT, Rhat)]

        for i, (j, g) in enumerate(chains):
            o = _dot(lhs[i], bd(s_ref[g]))
            s_ref[g] = o[:CHUNK] + QT[i]
            y_ref[0, rows_of(j), g * GROUP:(g + 1) * GROUP] = (o[CHUNK:] + Yloc[i]).astype(y_ref.dtype)
        return carry

    lax.fori_loop(0, tile // (CHUNK * chunks_per_iter), chunk_body, 0)


def _wkv(r, k, v, kk, b, lw, *, tile=256, chunks_per_iter=4):
    B, T, D = r.shape
    n_groups = D // GROUP
    spec = pl.BlockSpec((1, tile, D), lambda i, j: (i, j, 0))
    return pl.pallas_call(
        functools.partial(_wkv_kernel, tile=tile, n_groups=n_groups, chunks_per_iter=chunks_per_iter),
        grid=(B, T // tile),
        in_specs=[spec] * 6,
        out_specs=spec,
        out_shape=jax.ShapeDtypeStruct((B, T, D), F32),
        scratch_shapes=[pltpu.VMEM((n_groups, HEAD, GROUP), F32)],
        compiler_params=pltpu.CompilerParams(
            dimension_semantics=("arbitrary", "arbitrary"), vmem_limit_bytes=VMEM_LIMIT),
        name="wkv",
    )(r, k, v, kk, b, lw)


def _head_ones():
    return jnp.where(lax.broadcasted_iota(jnp.int32, (GROUP, GROUP), 0) // HEAD
                     == lax.broadcasted_iota(jnp.int32, (GROUP, GROUP), 1) // HEAD, 1.0, 0.0).astype(BF)


def _head_sum(q, ones_bd):
    hi = q.astype(BF)
    lo = (q - hi.astype(F32)).astype(BF)
    outs = []
    for g in range(q.shape[1] // GROUP):
        cols = slice(g * GROUP, (g + 1) * GROUP)
        outs.append(_dot(hi[:, cols], ones_bd) + _dot(lo[:, cols], ones_bd))
    return jnp.concatenate(outs, axis=1)


def _rms(x, gain):
    return x * lax.rsqrt(jnp.mean(x * x, axis=-1, keepdims=True) + RMS_EPS) * gain


(V_PRE, V_MU_R, V_MU_K, V_MU_V, V_MU_W, V_MU_A, V_MU_G, V_MU_VR, V_W0, V_A0, V_V0, V_KK, V_KA,
 V_RK) = range(14)
N_VEC_IN = 16
V_GNW, V_GNB, V_POSTMIX, V_PREFFN, V_POSTFFN, V_CB, V_LNW, V_LNB = range(8)
N_VEC_OUT = 8


def _mix_in_kernel(*refs, tile, d, has_vres):
    it = iter(refs)
    x_ref = next(it)
    vf_ref = next(it) if has_vres else None
    vec_ref, w_in_ref, w1_ref, w2_ref, a1_ref, a2_ref, g1_ref, g2_ref = (next(it) for _ in range(8))
    v1_ref, v2_ref = (next(it), next(it)) if has_vres else (None, None)
    (r_out, k_out, v_out, kk_out, b_out, lw_out, g_out, bonus_out, c_out, zr_out, zc_out) = (
        next(it) for _ in range(11))
    hprev_ref, pprev_ref = next(it), next(it)

    @pl.when(pl.program_id(1) == 0)
    def _():
        hprev_ref[...] = jnp.zeros_like(hprev_ref)
        pprev_ref[...] = jnp.zeros_like(pprev_ref)

    def vec(i):
        return vec_ref[i:i + 1, :]

    first_row = lax.broadcasted_iota(jnp.int32, (tile, 1), 0) == 0

    def shifted(cur, prev_row):
        return jnp.where(first_row, prev_row, pltpu.roll(cur, 1, axis=0))

    x = x_ref[0]
    h = _rms(x, vec(V_PRE))
    xx = shifted(h, hprev_ref[0:1, :]) - h
    hprev_ref[0:1, :] = h[tile - 1:tile, :]
    hb = h.astype(BF)

    def proj(col_block):
        return _dot(hb, w_in_ref[:, col_block * d:(col_block + 1) * d])

    def lerp(col_block, mu_row):
        p = proj(col_block)
        pp = shifted(p, pprev_ref[0:1, col_block * d:(col_block + 1) * d])
        pprev_ref[0:1, col_block * d:(col_block + 1) * d] = p[tile - 1:tile, :]
        return p + (pp - p) * vec(mu_row)

    def lora_in(mu_row, w_ref):
        return _dot((h + xx * vec(mu_row)).astype(BF), w_ref[...])

    lo_w = lora_in(V_MU_W, w1_ref)
    r = lerp(0, V_MU_R)
    lo_a = lora_in(V_MU_A, a1_ref)
    k = lerp(1, V_MU_K)
    lo_g = lora_in(V_MU_G, g1_ref)
    v = lerp(2, V_MU_V)
    if has_vres:
        lo_v = lora_in(V_MU_VR, v1_ref)

    pc_u = proj(3)
    z = vec(V_W0) + _dot(jnp.tanh(lo_w).astype(BF), w2_ref[...])
    softplus_neg = jnp.maximum(-z, 0.0) + jnp.log1p(jnp.exp(-jnp.abs(z)))
    lw_out[0] = -jnp.exp(-softplus_neg - 0.5)

    pc_g = proj(4)
    a = _sigmoid(vec(V_A0) + _dot(lo_a.astype(BF), a2_ref[...]))
    g_out[0] = _dot(_sigmoid(lo_g).astype(BF), g2_ref[...]).astype(BF)
    if has_vres:
        mix = _sigmoid(vec(V_V0) + _dot(lo_v.astype(BF), v2_ref[...]))
        v = v + (vf_ref[0].astype(F32) - v) * mix

    z_r = proj(5)
    ones_bd = _head_ones()
    kk = k * vec(V_KK)
    kk = kk / jnp.maximum(jnp.sqrt(_head_sum(kk * kk, ones_bd)), 1e-12)
    k = k * (1.0 + (a - 1.0) * vec(V_KA))
    r_out[0] = r.astype(BF)
    k_out[0] = k.astype(BF)
    v_out[0] = v.astype(BF)
    kk_out[0] = kk.astype(BF)
    b_out[0] = (kk * a).astype(BF)
    z_c = proj(6)
    bonus_out[0] = (_head_sum(r * k * vec(V_RK), ones_bd) * v).astype(BF)

    c_out[0] = (pc_u * _sigmoid(pc_g)).astype(BF)
    zr_out[0] = _sigmoid(z_r).astype(BF)
    zc_out[0] = _sigmoid(z_c).astype(BF)


def _mix_in(x, v_first, vecs, w_in, loras, *, tile=256):
    B, T, D = x.shape
    has_vres = v_first is not None
    tok = pl.BlockSpec((1, tile, D), lambda i, j: (i, j, 0))

    def const_spec(shape):
        return pl.BlockSpec(shape, lambda i, j: (0,) * len(shape), pipeline_mode=pl.Buffered(1))

    n_tok = 2 if has_vres else 1
    args = [x] + ([v_first] if has_vres else []) + [vecs, w_in] + list(loras)
    in_specs = [tok] * n_tok + [const_spec(a.shape) for a in args[n_tok:]]
    out_shape = [jax.ShapeDtypeStruct((B, T, D), F32 if i == 5 else BF) for i in range(11)]
    return pl.pallas_call(
        functools.partial(_mix_in_kernel, tile=tile, d=D, has_vres=has_vres),
        grid=(B, T // tile),
        in_specs=in_specs,
        out_specs=[tok] * 11,
        out_shape=out_shape,
        scratch_shapes=[pltpu.VMEM((SUBLANES, D), F32), pltpu.VMEM((SUBLANES, 3 * D), F32)],
        compiler_params=pltpu.CompilerParams(
            dimension_semantics=("arbitrary", "arbitrary"), vmem_limit_bytes=VMEM_LIMIT),
        name="mix_in",
    )(*args)


def _mix_out_kernel(x_ref, y_ref, g_ref, bonus_ref, c_ref, zr_ref, zc_ref, vec_ref, dw_ref, wo_ref, wco_ref,
                    wout_ref, wg_ref, wu_ref, wd_ref, o_ref, cbuf_ref, cacc_ref, x1_ref, h2_ref, *,
                    tile, d, tiles_per_seq):
    step = pl.program_id(0)

    def vec(i):
        return vec_ref[i:i + 1, :]

    @pl.when(step == 0)
    def _():
        x1_ref[...] = jnp.zeros_like(x1_ref)
        h2_ref[...] = jnp.zeros_like(h2_ref)

    @pl.when(step % tiles_per_seq == 0)
    def _():
        cbuf_ref[0:HALO, :] = jnp.zeros((HALO, d), F32)

    ones_bd = _head_ones()
    y = y_ref[0]
    yc = y - _head_sum(y, ones_bd) * (1.0 / HEAD)
    var = _head_sum(yc * yc, ones_bd) * (1.0 / HEAD)
    yn = yc * lax.rsqrt(var + GN_EPS) * vec(V_GNW) + vec(V_GNB)
    y_rwkv = _dot(((yn + bonus_ref[0].astype(F32)) * g_ref[0].astype(F32)).astype(BF), wo_ref[...])

    cbuf_ref[HALO:HALO + tile, :] = c_ref[0].astype(F32)
    rb, lb = 64, 128

    def conv_block(l):
        cols = slice(l * lb, (l + 1) * lb)
        for m in range(SUBLANES):
            taps = [(q, dw_ref[CONV_K - 1 - (SUBLANES * q + m):CONV_K - (SUBLANES * q + m), cols])
                    for q in range((CONV_K - 1 - m) // SUBLANES + 1)]
            for i in range(tile // rb):
                pm = None
                for q, w in taps:
                    lo = HALO + i * rb - SUBLANES * (q + 1)
                    term = w * cbuf_ref[lo:lo + rb + SUBLANES, cols]
                    pm = term if pm is None else pm + term
                if m:
                    pm = pltpu.roll(pm, m, axis=0)
                    cacc_ref[i * rb:(i + 1) * rb, cols] += pm[SUBLANES:, :]
                else:
                    cacc_ref[i * rb:(i + 1) * rb, cols] = pm[SUBLANES:, :] + vec_ref[V_CB:V_CB + 1, cols]

    conv_blocks = [(l,) for l in range(d // lb)]

    x1_prev = x1_ref[...]
    hb = h2_ref[...]
    d_ff = wg_ref.shape[1]
    bounds = [min(n * FFN_CHUNK, d_ff) for n in range(-(-d_ff // FFN_CHUNK) + 1)]
    per_chunk = -(-len(conv_blocks) // (len(bounds) - 1))
    f = None
    for n, (lo_c, hi_c) in enumerate(zip(bounds[:-1], bounds[1:])):
        gate = _dot(hb, wg_ref[:, lo_c:hi_c])
        up = _dot(hb, wu_ref[:, lo_c:hi_c])
        for blk in conv_blocks[n * per_chunk:(n + 1) * per_chunk]:
            conv_block(*blk)
        part = _dot((gate * _sigmoid(gate) * up).astype(BF), wd_ref[lo_c:hi_c, :])
        f = part if f is None else f + part
    o_ref[0] = x1_prev + _rms(f, vec(V_POSTFFN))
    cbuf_ref[0:HALO, :] = cbuf_ref[tile:tile + HALO, :]
    c = cacc_ref[...]
    mu = jnp.mean(c, axis=-1, keepdims=True)
    cz = c - mu
    var = jnp.mean(cz * cz, axis=-1, keepdims=True)
    cn = cz * lax.rsqrt(var + LN_EPS) * vec(V_LNW) + vec(V_LNB)
    y_conv = _dot((cn * _sigmoid(cn)).astype(BF), wco_ref[...])

    merged = zr_ref[0].astype(F32) * y_rwkv + zc_ref[0].astype(F32) * y_conv
    x1 = x_ref[0] + _rms(_dot(merged.astype(BF), wout_ref[...]), vec(V_POSTMIX))
    x1_ref[...] = x1
    h2_ref[...] = _rms(x1, vec(V_PREFFN)).astype(BF)


def _mix_out(x, y, g, bonus, c, zr, zc, vecs, dw, w_o, w_conv_out, w_out, w_gate, w_up, w_down, *, tile=256):
    B, T, D = x.shape
    tiles_per_seq = T // tile
    n_tiles = B * tiles_per_seq

    def tok_in(s):
        s = jnp.minimum(s, n_tiles - 1)
        return (s // tiles_per_seq, s % tiles_per_seq, 0)

    def tok_out(s):
        s = jnp.maximum(s - 1, 0)
        return (s // tiles_per_seq, s % tiles_per_seq, 0)

    def const_spec(shape):
        return pl.BlockSpec(shape, lambda s: (0,) * len(shape), pipeline_mode=pl.Buffered(1))

    consts = [vecs, dw, w_o, w_conv_out, w_out, w_gate, w_up, w_down]
    return pl.pallas_call(
        functools.partial(_mix_out_kernel, tile=tile, d=D, tiles_per_seq=tiles_per_seq),
        grid=(n_tiles + 1,),
        in_specs=[pl.BlockSpec((1, tile, D), tok_in)] * 7 + [const_spec(a.shape) for a in consts],
        out_specs=pl.BlockSpec((1, tile, D), tok_out),
        out_shape=jax.ShapeDtypeStruct((B, T, D), F32),
        scratch_shapes=[pltpu.VMEM((tile + HALO, D), F32), pltpu.VMEM((tile, D), F32),
                        pltpu.VMEM((tile, D), F32), pltpu.VMEM((tile, D), BF)],
        compiler_params=pltpu.CompilerParams(
            dimension_semantics=("arbitrary",), vmem_limit_bytes=VMEM_LIMIT),
        name="mix_out",
    )(x, y, g, bonus, c, zr, zc, *consts)


def _rows(rows, n, d):
    rows = [jnp.reshape(r, (1, d)).astype(F32) for r in rows]
    return jnp.concatenate(rows + [jnp.zeros((n - len(rows), d), F32)], axis=0)


def kernel(x, pre_mix_norm, post_mix_norm, pre_ffn_norm, post_ffn_norm, w_in, mu_rkv, mu_wag, decay_w0, decay_w1, decay_w2, a_0, a_1, a_2, g_1, g_2, vres_mu, vres_0, vres_1, vres_2, k_k, k_a, r_k, gn_w, gn_b, w_rwkv_out, conv_dw, conv_b, conv_ln_w, conv_ln_b, w_conv_out, w_out, ffn_w_gate, ffn_w_up, ffn_w_down):
    depth, d = pre_mix_norm.shape
    bf = lambda w: w.astype(BF)
    v_first = None
    for i in range(depth):
        has_vres = i > 0
        zero = jnp.zeros((d,), F32)
        vec_in = _rows([pre_mix_norm[i], mu_rkv[i, 0], mu_rkv[i, 1], mu_rkv[i, 2],
                        mu_wag[i, 0], mu_wag[i, 1], mu_wag[i, 2],
                        vres_mu[i - 1] if has_vres else zero, decay_w0[i], a_0[i],
                        vres_0[i - 1] if has_vres else zero, k_k[i], k_a[i], r_k[i]], N_VEC_IN, d)
        loras = [bf(decay_w1[i]), bf(decay_w2[i]), bf(a_1[i]), bf(a_2[i]), bf(g_1[i]), bf(g_2[i])]
        if has_vres:
            loras += [bf(vres_1[i - 1]), bf(vres_2[i - 1])]
        r, k, v, kk, b, lw, g, bonus, c, zr, zc = _mix_in(x, v_first, vec_in, bf(w_in[i]), loras)
        if not has_vres:
            v_first = v
        y = _wkv(r, k, v, kk, b, lw)
        vec_out = _rows([gn_w[i], gn_b[i], post_mix_norm[i], pre_ffn_norm[i], post_ffn_norm[i],
                         conv_b[i], conv_ln_w[i], conv_ln_b[i]], N_VEC_OUT, d)
        x = _mix_out(x, y, g, bonus, c, zr, zc, vec_out, conv_dw[i], bf(w_rwkv_out[i]), bf(w_conv_out[i]),
                     bf(w_out[i]), bf(ffn_w_gate[i]), bf(ffn_w_up[i]), bf(ffn_w_down[i]))
    return x
```

```python
import functools

import jax
import jax.numpy as jnp
from jax import lax
from jax.experimental import pallas as pl
from jax.experimental.pallas import tpu as pltpu

BF = jnp.bfloat16
F32 = jnp.float32

HEAD = 64
GROUP = 256
CHUNK = 64
CONV_K = 31
SUBLANES = 8
FFN_CHUNK = 768
HALO = 32
RMS_EPS = 1e-6
LN_EPS = 1e-5
GN_EPS = 64e-5
VMEM_LIMIT = 56 * 1024 * 1024


def _dot(a, b):
    return jnp.dot(a, b, preferred_element_type=F32)


def _dot_nt(a, b):
    return lax.dot_general(a, b, (((1,), (1,)), ((), ())), preferred_element_type=F32)


def _sigmoid(x):
    return 1.0 / (1.0 + jnp.exp(-x))


def _wkv_kernel(r_ref, k_ref, v_ref, kk_ref, b_ref, lw_ref, y_ref, s_ref, *, tile, n_groups,
                chunks_per_iter, tiles_per_seq):
    @pl.when(pl.program_id(0) % tiles_per_seq == 0)
    def _():
        s_ref[...] = jnp.zeros_like(s_ref)

    row = lax.broadcasted_iota(jnp.int32, (CHUNK, GROUP), 0)
    col = lax.broadcasted_iota(jnp.int32, (CHUNK, GROUP), 1) % HEAD
    m_strict = col < row
    m_incl = col <= row
    eye = jnp.where(col == row, 1.0, 0.0).astype(F32)
    eye_bf = eye.astype(BF)
    bdm = (lax.broadcasted_iota(jnp.int32, (GROUP, GROUP), 0) // HEAD
           == lax.broadcasted_iota(jnp.int32, (GROUP, GROUP), 1) // HEAD)
    tri = jnp.where(lax.broadcasted_iota(jnp.int32, (CHUNK, CHUNK), 0)
                    >= lax.broadcasted_iota(jnp.int32, (CHUNK, CHUNK), 1), 1.0, 0.0).astype(BF)

    def bd(x):
        xb = x.astype(BF)
        return jnp.where(bdm, jnp.concatenate([xb] * (GROUP // HEAD), axis=0), jnp.zeros((), BF))

    def mm(a, b_bf):
        return _dot(a.astype(BF), b_bf)

    def stack(a, b):
        return jnp.concatenate([a, b], axis=0)

    def chunk_body(it, carry):
        chains = [(j, g) for j in range(chunks_per_iter) for g in range(n_groups)]

        def rows_of(j):
            return pl.ds(pl.multiple_of((it * chunks_per_iter + j) * CHUNK, CHUNK), CHUNK)

        def load(ref):
            return [ref[0, rows_of(j), g * GROUP:(g + 1) * GROUP] for j, g in chains]

        lw = load(lw_ref)
        p0 = [x.astype(BF) for x in lw]
        r0 = [x - p.astype(F32) for x, p in zip(lw, p0)]
        p1 = [x.astype(BF) for x in r0]
        p2 = [(x - p.astype(F32)).astype(BF) for x, p in zip(r0, p1)]
        cum = [_dot(tri, a) + _dot(tri, b) + _dot(tri, c) for a, b, c in zip(p0, p1, p2)]
        cum_end = [x[CHUNK - 1:CHUNK, :] for x in cum]
        r = [x.astype(F32) for x in load(r_ref)]
        k = [x.astype(F32) for x in load(k_ref)]
        v = load(v_ref)
        kk = [x.astype(F32) for x in load(kk_ref)]
        b = [x.astype(F32) for x in load(b_ref)]
        einv = [jnp.exp(-x) for x in cum]
        eend = [jnp.exp(ce - x) for x, ce in zip(cum, cum_end)]
        Rt = [x * jnp.exp(c) for x, c in zip(r, cum)]
        At = [-x * jnp.exp(c - l) for x, c, l in zip(kk, cum, lw)]
        Kt = [x * e for x, e in zip(k, einv)]
        Bt = [x * e for x, e in zip(b, einv)]
        Kp = [x * e for x, e in zip(k, eend)]
        Bp = [x * e for x, e in zip(b, eend)]

        AR = [stack(a, x).astype(BF) for a, x in zip(At, Rt)]
        Gb = [_dot_nt(a, bd(x)) for a, x in zip(AR, Bt)]
        Gk = [_dot_nt(a, bd(x)) for a, x in zip(AR, Kt)]
        N = [jnp.where(m_strict, x[:CHUNK], 0.0) for x in Gb]
        Aak = [jnp.where(m_strict, x[:CHUNK], 0.0) for x in Gk]
        Arb = [jnp.where(m_incl, x[CHUNK:], 0.0) for x in Gb]
        Ark = [jnp.where(m_incl, x[CHUNK:], 0.0) for x in Gk]

        T = [eye + x for x in N]
        Pw = [mm(x, bd(x)) for x in N]
        for _ in range(4):
            o = [mm(stack(p, t), bd(p)) for p, t in zip(Pw, T)]
            Pw = [x[:CHUNK] for x in o]
            T = [t + x[CHUNK:] for t, x in zip(T, o)]
        T = [t + mm(t, bd(p)) for t, p in zip(T, Pw)]

        BpT = [_dot_nt(eye_bf, bd(x)) for x in Bp]
        KpT = [_dot_nt(eye_bf, bd(x)) for x in Kp]

        o = [mm(jnp.concatenate([a, x, kt], axis=0), bd(vv)) for a, x, kt, vv in zip(Aak, Ark, KpT, v)]
        AakV = [x[:CHUNK] for x in o]
        ArkV = [x[CHUNK:2 * CHUNK] for x in o]
        KpV = [x[2 * CHUNK:] for x in o]
        Wt = [mm(t, bd(a)) for t, a in zip(T, At)]
        Ut = [mm(t, bd(a)) for t, a in zip(T, AakV)]
        AB = [stack(a, x).astype(BF) for a, x in zip(Arb, BpT)]
        oW = [_dot(a, bd(w)) for a, w in zip(AB, Wt)]
        oU = [_dot(a, bd(u)) for a, u in zip(AB, Ut)]
        Rhat = [x + y[:CHUNK] for x, y in zip(Rt, oW)]
        PT = [y[CHUNK:] + eye * jnp.exp(ce) for y, ce in zip(oW, cum_end)]
        Yloc = [y[:CHUNK] + x for y, x in zip(oU, ArkV)]
        QT = [y[CHUNK:] + x for y, x in zip(oU, KpV)]
        lhs = [stack(p, x).astype(BF) for p, x in zip(PT, Rhat)]

        for i, (j, g) in enumerate(chains):
            o = _dot(lhs[i], bd(s_ref[g]))
            s_ref[g] = o[:CHUNK] + QT[i]
            y_ref[0, rows_of(j), g * GROUP:(g + 1) * GROUP] = (o[CHUNK:] + Yloc[i]).astype(y_ref.dtype)
        return carry

    lax.fori_loop(0, tile // (CHUNK * chunks_per_iter), chunk_body, 0)


def _wkv(r, k, v, kk, b, lw, *, tile=512, chunks_per_iter=4):
    B, T, D = r.shape
    assert T % tile == 0 and tile % (CHUNK * chunks_per_iter) == 0 and D % GROUP == 0
    n_groups = D // GROUP
    tiles_per_seq = T // tile
    spec = pl.BlockSpec((1, tile, D), lambda s: (s // tiles_per_seq, s % tiles_per_seq, 0))
    return pl.pallas_call(
        functools.partial(_wkv_kernel, tile=tile, n_groups=n_groups, chunks_per_iter=chunks_per_iter,
                          tiles_per_seq=tiles_per_seq),
        grid=(B * tiles_per_seq,),
        in_specs=[spec] * 6,
        out_specs=spec,
        out_shape=jax.ShapeDtypeStruct((B, T, D), F32),
        scratch_shapes=[pltpu.VMEM((n_groups, HEAD, GROUP), F32)],
        compiler_params=pltpu.CompilerParams(
            dimension_semantics=("arbitrary",), vmem_limit_bytes=VMEM_LIMIT),
        name="wkv",
    )(r, k, v, kk, b, lw)


def _head_ones():
    return jnp.where(lax.broadcasted_iota(jnp.int32, (GROUP, GROUP), 0) // HEAD
                     == lax.broadcasted_iota(jnp.int32, (GROUP, GROUP), 1) // HEAD, 1.0, 0.0).astype(BF)


def _head_sum(q, ones_bd):
    hi = q.astype(BF)
    lo = (q - hi.astype(F32)).astype(BF)
    outs = []
    for g in range(q.shape[1] // GROUP):
        cols = slice(g * GROUP, (g + 1) * GROUP)
        outs.append(_dot(hi[:, cols], ones_bd) + _dot(lo[:, cols], ones_bd))
    return jnp.concatenate(outs, axis=1)


def _rms(x, gain):
    return x * lax.rsqrt(jnp.mean(x * x, axis=-1, keepdims=True) + RMS_EPS) * gain


(V_PRE, V_MU_R, V_MU_K, V_MU_V, V_MU_W, V_MU_A, V_MU_G, V_MU_VR, V_W0, V_A0, V_V0, V_KK, V_KA,
 V_RK) = range(14)
N_VEC_IN = 16
V_GNW, V_GNB, V_POSTMIX, V_PREFFN, V_POSTFFN, V_CB, V_LNW, V_LNB = range(8)
N_VEC_OUT = 8


def _mix_in_kernel(*refs, tile, d, has_vres, tiles_per_seq):
    it = iter(refs)
    x_ref = next(it)
    vf_ref = next(it) if has_vres else None
    vec_ref, w_in_ref, w1_ref, w2_ref, a1_ref, a2_ref, g1_ref, g2_ref = (next(it) for _ in range(8))
    v1_ref, v2_ref = (next(it), next(it)) if has_vres else (None, None)
    (r_out, k_out, v_out, kk_out, b_out, lw_out, g_out, bonus_out, c_out, zr_out, zc_out) = (
        next(it) for _ in range(11))
    hprev_ref, pprev_ref = next(it), next(it)

    @pl.when(pl.program_id(0) % tiles_per_seq == 0)
    def _():
        hprev_ref[...] = jnp.zeros_like(hprev_ref)
        pprev_ref[...] = jnp.zeros_like(pprev_ref)

    def vec(i):
        return vec_ref[i:i + 1, :]

    first_row = lax.broadcasted_iota(jnp.int32, (tile, 1), 0) == 0

    def shifted(cur, prev_row):
        return jnp.where(first_row, prev_row, pltpu.roll(cur, 1, axis=0))

    x = x_ref[0]
    h = _rms(x, vec(V_PRE))
    xx = shifted(h, hprev_ref[0:1, :]) - h
    hprev_ref[0:1, :] = h[tile - 1:tile, :]
    hb = h.astype(BF)

    def proj(col_block):
        return _dot(hb, w_in_ref[:, col_block * d:(col_block + 1) * d])

    def lerp(col_block, mu_row):
        p = proj(col_block)
        pp = shifted(p, pprev_ref[0:1, col_block * d:(col_block + 1) * d])
        pprev_ref[0:1, col_block * d:(col_block + 1) * d] = p[tile - 1:tile, :]
        return p + (pp - p) * vec(mu_row)

    def lora_in(mu_row, w_ref):
        return _dot((h + xx * vec(mu_row)).astype(BF), w_ref[...])

    lo_w = lora_in(V_MU_W, w1_ref)
    r = lerp(0, V_MU_R)
    lo_a = lora_in(V_MU_A, a1_ref)
    k = lerp(1, V_MU_K)
    lo_g = lora_in(V_MU_G, g1_ref)
    v = lerp(2, V_MU_V)
    if has_vres:
        lo_v = lora_in(V_MU_VR, v1_ref)

    pc_u = proj(3)
    z = vec(V_W0) + _dot(jnp.tanh(lo_w).astype(BF), w2_ref[...])
    softplus_neg = jnp.maximum(-z, 0.0) + jnp.log1p(jnp.exp(-jnp.abs(z)))
    lw_out[0] = -jnp.exp(-softplus_neg - 0.5)

    pc_g = proj(4)
    a = _sigmoid(vec(V_A0) + _dot(lo_a.astype(BF), a2_ref[...]))
    g_out[0] = _dot(_sigmoid(lo_g).astype(BF), g2_ref[...]).astype(BF)
    if has_vres:
        mix = _sigmoid(vec(V_V0) + _dot(lo_v.astype(BF), v2_ref[...]))
        v = v + (vf_ref[0].astype(F32) - v) * mix

    z_r = proj(5)
    ones_bd = _head_ones()
    kk = k * vec(V_KK)
    kk = kk / jnp.maximum(jnp.sqrt(_head_sum(kk * kk, ones_bd)), 1e-12)
    k = k * (1.0 + (a - 1.0) * vec(V_KA))
    r_out[0] = r.astype(BF)
    k_out[0] = k.astype(BF)
    v_out[0] = v.astype(BF)
    kk_out[0] = kk.astype(BF)
    b_out[0] = (kk * a).astype(BF)
    z_c = proj(6)
    bonus_out[0] = (_head_sum(r * k * vec(V_RK), ones_bd) * v).astype(BF)

    c_out[0] = (pc_u * _sigmoid(pc_g)).astype(BF)
    zr_out[0] = _sigmoid(z_r).astype(BF)
    zc_out[0] = _sigmoid(z_c).astype(BF)


def _mix_in(x, v_first, vecs, w_in, loras, *, tile=256):
    B, T, D = x.shape
    has_vres = v_first is not None
    assert T % tile == 0
    tiles_per_seq = T // tile
    tok = pl.BlockSpec((1, tile, D), lambda s: (s // tiles_per_seq, s % tiles_per_seq, 0))

    def const_spec(shape):
        return pl.BlockSpec(shape, lambda s: (0,) * len(shape), pipeline_mode=pl.Buffered(1))

    n_tok = 2 if has_vres else 1
    args = [x] + ([v_first] if has_vres else []) + [vecs, w_in] + list(loras)
    in_specs = [tok] * n_tok + [const_spec(a.shape) for a in args[n_tok:]]
    out_shape = [jax.ShapeDtypeStruct((B, T, D), F32 if i == 5 else BF) for i in range(11)]
    return pl.pallas_call(
        functools.partial(_mix_in_kernel, tile=tile, d=D, has_vres=has_vres, tiles_per_seq=tiles_per_seq),
        grid=(B * tiles_per_seq,),
        in_specs=in_specs,
        out_specs=[tok] * 11,
        out_shape=out_shape,
        scratch_shapes=[pltpu.VMEM((SUBLANES, D), F32), pltpu.VMEM((SUBLANES, 3 * D), F32)],
        compiler_params=pltpu.CompilerParams(
            dimension_semantics=("arbitrary",), vmem_limit_bytes=VMEM_LIMIT),
        name="mix_in",
    )(*args)


def _mix_out_kernel(x_ref, y_ref, g_ref, bonus_ref, c_ref, zr_ref, zc_ref, vec_ref, dw_ref, wo_ref, wco_ref,
                    wout_ref, wg_ref, wu_ref, wd_ref, o_ref, cbuf_ref, cacc_ref, x1_ref, h2_ref,
                    xprev_ref, hprev_ref, *, tile, d, tiles_per_seq):
    step = pl.program_id(0)

    def vec(i):
        return vec_ref[i:i + 1, :]

    @pl.when(step == 0)
    def _():
        x1_ref[...] = jnp.zeros_like(x1_ref)
        h2_ref[...] = jnp.zeros_like(h2_ref)

    @pl.when(step % tiles_per_seq == 0)
    def _():
        cbuf_ref[0:HALO, :] = jnp.zeros((HALO, d), F32)

    ones_bd = _head_ones()
    y = y_ref[0]
    yc = y - _head_sum(y, ones_bd) * (1.0 / HEAD)
    var = _head_sum(yc * yc, ones_bd) * (1.0 / HEAD)
    yn = yc * lax.rsqrt(var + GN_EPS) * vec(V_GNW) + vec(V_GNB)
    y_rwkv = _dot(((yn + bonus_ref[0].astype(F32)) * g_ref[0].astype(F32)).astype(BF), wo_ref[...])

    cbuf_ref[HALO:HALO + tile, :] = c_ref[0].astype(F32)
    rb, lb = 64, 128

    def conv_block(l):
        cols = slice(l * lb, (l + 1) * lb)
        for m in range(SUBLANES):
            taps = [(q, dw_ref[CONV_K - 1 - (SUBLANES * q + m):CONV_K - (SUBLANES * q + m), cols])
                    for q in range((CONV_K - 1 - m) // SUBLANES + 1)]
            for i in range(tile // rb):
                pm = None
                for q, w in taps:
                    lo = HALO + i * rb - SUBLANES * (q + 1)
                    term = w * cbuf_ref[lo:lo + rb + SUBLANES, cols]
                    pm = term if pm is None else pm + term
                if m:
                    pm = pltpu.roll(pm, m, axis=0)
                    cacc_ref[i * rb:(i + 1) * rb, cols] += pm[SUBLANES:, :]
                else:
                    cacc_ref[i * rb:(i + 1) * rb, cols] = pm[SUBLANES:, :] + vec_ref[V_CB:V_CB + 1, cols]

    for l in range(d // lb):
        conv_block(l)
    cbuf_ref[0:HALO, :] = cbuf_ref[tile:tile + HALO, :]

    xprev_ref[...] = x1_ref[...]
    hprev_ref[...] = h2_ref[...]
    d_ff = wg_ref.shape[1]
    bounds = [min(n * FFN_CHUNK, d_ff) for n in range(-(-d_ff // FFN_CHUNK) + 1)]
    chunks = list(zip(bounds[:-1], bounds[1:]))

    def ffn_chunk(lo_c, hi_c):
        hb = hprev_ref[...]
        gate = _dot(hb, wg_ref[:, lo_c:hi_c])
        up = _dot(hb, wu_ref[:, lo_c:hi_c])
        return _dot((gate * _sigmoid(gate) * up).astype(BF), wd_ref[lo_c:hi_c, :])

    half = len(chunks) // 2
    f = None
    for lo_c, hi_c in chunks[:half]:
        part = ffn_chunk(lo_c, hi_c)
        f = part if f is None else f + part

    c = cacc_ref[...]
    mu = jnp.mean(c, axis=-1, keepdims=True)
    cz = c - mu
    var = jnp.mean(cz * cz, axis=-1, keepdims=True)
    cn = cz * lax.rsqrt(var + LN_EPS) * vec(V_LNW) + vec(V_LNB)
    y_conv = _dot((cn * _sigmoid(cn)).astype(BF), wco_ref[...])
    merged = zr_ref[0].astype(F32) * y_rwkv + zc_ref[0].astype(F32) * y_conv
    x1 = x_ref[0] + _rms(_dot(merged.astype(BF), wout_ref[...]), vec(V_POSTMIX))
    x1_ref[...] = x1
    h2_ref[...] = _rms(x1, vec(V_PREFFN)).astype(BF)

    for lo_c, hi_c in chunks[half:]:
        f = f + ffn_chunk(lo_c, hi_c)
    o_ref[0] = xprev_ref[...] + _rms(f, vec(V_POSTFFN))


def _mix_out(x, y, g, bonus, c, zr, zc, vecs, dw, w_o, w_conv_out, w_out, w_gate, w_up, w_down, *, tile=256):
    B, T, D = x.shape
    assert T % tile == 0
    tiles_per_seq = T // tile
    n_tiles = B * tiles_per_seq

    def tok_in(s):
        s = jnp.minimum(s, n_tiles - 1)
        return (s // tiles_per_seq, s % tiles_per_seq, 0)

    def tok_out(s):
        s = jnp.maximum(s - 1, 0)
        return (s // tiles_per_seq, s % tiles_per_seq, 0)

    def const_spec(shape):
        return pl.BlockSpec(shape, lambda s: (0,) * len(shape), pipeline_mode=pl.Buffered(1))

    consts = [vecs, dw, w_o, w_conv_out, w_out, w_gate, w_up, w_down]
    return pl.pallas_call(
        functools.partial(_mix_out_kernel, tile=tile, d=D, tiles_per_seq=tiles_per_seq),
        grid=(n_tiles + 1,),
        in_specs=[pl.BlockSpec((1, tile, D), tok_in)] * 7 + [const_spec(a.shape) for a in consts],
        out_specs=pl.BlockSpec((1, tile, D), tok_out),
        out_shape=jax.ShapeDtypeStruct((B, T, D), F32),
        scratch_shapes=[pltpu.VMEM((tile + HALO, D), F32), pltpu.VMEM((tile, D), F32),
                        pltpu.VMEM((tile, D), F32), pltpu.VMEM((tile, D), BF),
                        pltpu.VMEM((tile, D), F32), pltpu.VMEM((tile, D), BF)],
        compiler_params=pltpu.CompilerParams(
            dimension_semantics=("arbitrary",), vmem_limit_bytes=VMEM_LIMIT),
        name="mix_out",
    )(x, y, g, bonus, c, zr, zc, *consts)


def _rows(rows, n, d):
    rows = [jnp.reshape(r, (1, d)).astype(F32) for r in rows]
    return jnp.concatenate(rows + [jnp.zeros((n - len(rows), d), F32)], axis=0)


def kernel(x, pre_mix_norm, post_mix_norm, pre_ffn_norm, post_ffn_norm, w_in, mu_rkv, mu_wag, decay_w0, decay_w1, decay_w2, a_0, a_1, a_2, g_1, g_2, vres_mu, vres_0, vres_1, vres_2, k_k, k_a, r_k, gn_w, gn_b, w_rwkv_out, conv_dw, conv_b, conv_ln_w, conv_ln_b, w_conv_out, w_out, ffn_w_gate, ffn_w_up, ffn_w_down):
    depth, d = pre_mix_norm.shape
    bf = lambda w: w.astype(BF)
    v_first = None
    for i in range(depth):
        has_vres = i > 0
        zero = jnp.zeros((d,), F32)
        vec_in = _rows([pre_mix_norm[i], mu_rkv[i, 0], mu_rkv[i, 1], mu_rkv[i, 2],
                        mu_wag[i, 0], mu_wag[i, 1], mu_wag[i, 2],
                        vres_mu[i - 1] if has_vres else zero, decay_w0[i], a_0[i],
                        vres_0[i - 1] if has_vres else zero, k_k[i], k_a[i], r_k[i]], N_VEC_IN, d)
        loras = [bf(decay_w1[i]), bf(decay_w2[i]), bf(a_1[i]), bf(a_2[i]), bf(g_1[i]), bf(g_2[i])]
        if has_vres:
            loras += [bf(vres_1[i - 1]), bf(vres_2[i - 1])]
        r, k, v, kk, b, lw, g, bonus, c, zr, zc = _mix_in(x, v_first, vec_in, bf(w_in[i]), loras)
        if not has_vres:
            v_first = v
        y = _wkv(r, k, v, kk, b, lw)
        vec_out = _rows([gn_w[i], gn_b[i], post_mix_norm[i], pre_ffn_norm[i], post_ffn_norm[i],
                         conv_b[i], conv_ln_w[i], conv_ln_b[i]], N_VEC_OUT, d)
        x = _mix_out(x, y, g, bonus, c, zr, zc, vec_out, conv_dw[i], bf(w_rwkv_out[i]), bf(w_conv_out[i]),
                     bf(w_out[i]), bf(ffn_w_gate[i]), bf(ffn_w_up[i]), bf(ffn_w_down[i]))
    return x
```
